```python
import math
import jax, jax.numpy as jnp
from jax import lax
import numpy as np

D_MODEL = 1024
BATCH = 16
SEQ = 2048
DEPTH = 1

N_META = 16
GRID_W = 64
CONV_DIM = 1024
CONV_K = 31
N_HEADS = 16
N_KV_HEADS = 4
HEAD_DIM = 64
GQA_GROUP = N_HEADS // N_KV_HEADS
ATTN_DIM = N_HEADS * HEAD_DIM
KV_DIM = N_KV_HEADS * HEAD_DIM
ROPE_FREQS = HEAD_DIM // 4
ROPE_THETA = 10000.0
Q_BLOCK = 128
NORM_EPS = 1e-6

IN_SPLITS = [CONV_DIM, CONV_DIM, CONV_DIM,
             ATTN_DIM, KV_DIM, KV_DIM, ATTN_DIM,
             D_MODEL, D_MODEL]
IN_DIM = sum(IN_SPLITS)
IN_OFFSETS = list(np.cumsum(IN_SPLITS)[:-1].tolist())

kernel_name = "hybrid_conformer_gqa_gated_encoder"


def rms_norm(x, g, eps=NORM_EPS):
    xf = x.astype(jnp.float32)
    y = xf * lax.rsqrt(jnp.mean(xf * xf, axis=-1, keepdims=True) + eps)
    return (y * g.astype(jnp.float32)).astype(x.dtype)


def layer_norm(x, g, b, eps=NORM_EPS):
    xf = x.astype(jnp.float32)
    mu = jnp.mean(xf, axis=-1, keepdims=True)
    xc = xf - mu
    y = xc * lax.rsqrt(jnp.mean(xc * xc, axis=-1, keepdims=True) + eps)
    return (y * g.astype(jnp.float32) + b.astype(jnp.float32)).astype(x.dtype)


def rope_tables(n_tok):
    rows = n_tok // GRID_W
    row_ids = jnp.concatenate([jnp.zeros((N_META,), jnp.float32),
                               jnp.repeat(jnp.arange(rows, dtype=jnp.float32), GRID_W)])
    col_ids = jnp.concatenate([jnp.zeros((N_META,), jnp.float32),
                               jnp.tile(jnp.arange(GRID_W, dtype=jnp.float32), rows)])
    inv_freq = ROPE_THETA ** (-jnp.arange(ROPE_FREQS, dtype=jnp.float32) / ROPE_FREQS)
    a_row = row_ids[:, None] * inv_freq[None, :]
    a_col = col_ids[:, None] * inv_freq[None, :]
    ang = jnp.concatenate([a_row, a_row, a_col, a_col], axis=-1)
    return jnp.cos(ang), jnp.sin(ang)


def apply_rope2d(x, cos, sin):
    xs = x.reshape(x.shape[:-1] + (2, 2, ROPE_FREQS))
    rot = jnp.stack([-xs[..., 1, :], xs[..., 0, :]], axis=-2).reshape(x.shape)
    c = cos[None, :, None, :].astype(x.dtype)
    s = sin[None, :, None, :].astype(x.dtype)
    return x * c + rot * s


def conv_branch(val, glu_gate, z, conv_w, conv_b, cn_g, cn_b, w_proj):
    u = val * jax.nn.sigmoid(glu_gate)
    kern = conv_w.reshape(CONV_K, 1, CONV_DIM).astype(u.dtype)
    pad = CONV_K // 2
    c = lax.conv_general_dilated(u, kern, window_strides=(1,), padding=[(pad, pad)],
                                 dimension_numbers=("NWC", "WIO", "NWC"),
                                 feature_group_count=CONV_DIM)
    c = c + conv_b.astype(c.dtype)
    c = jax.nn.silu(layer_norm(c, cn_g, cn_b))
    c = c * jax.nn.silu(z)
    return jnp.einsum("blc,cd->bld", c, w_proj.astype(c.dtype))


def attn_branch(q, k, v, z, q_g, k_g, w_proj, cos, sin):
    B, L, _ = q.shape
    n_tok = L - N_META
    q = q.reshape(B, L, N_HEADS, HEAD_DIM)
    k = k.reshape(B, L, N_KV_HEADS, HEAD_DIM)
    v = v.reshape(B, L, N_KV_HEADS, HEAD_DIM)
    q = apply_rope2d(rms_norm(q, q_g), cos, sin)
    k = apply_rope2d(rms_norm(k, k_g), cos, sin)
    q = q.reshape(B, L, N_KV_HEADS, GQA_GROUP, HEAD_DIM)
    scale = 1.0 / math.sqrt(HEAD_DIM)

    def attend(qb):
        s = jnp.einsum("bqkgd,bskd->bkgqs", qb, k).astype(jnp.float32) * scale
        p = jax.nn.softmax(s, axis=-1).astype(v.dtype)
        return jnp.einsum("bkgqs,bskd->bqkgd", p, v)

    o_meta = attend(q[:, :N_META])
    n_blk = n_tok // Q_BLOCK
    q_real = q[:, N_META:].reshape(B, n_blk, Q_BLOCK, N_KV_HEADS, GQA_GROUP, HEAD_DIM)
    o_real = lax.map(attend, jnp.moveaxis(q_real, 1, 0))
    o_real = jnp.moveaxis(o_real, 0, 1).reshape(B, n_tok, N_KV_HEADS, GQA_GROUP, HEAD_DIM)
    o = jnp.concatenate([o_meta, o_real], axis=1).reshape(B, L, ATTN_DIM)
    o = o * jax.nn.silu(z)
    return jnp.einsum("bla,ad->bld", o, w_proj.astype(o.dtype))


def hybrid_layer(h, norm_g, w_in, conv_w, conv_b, cn_g, cn_b, w_conv_out,
                 q_g, k_g, w_attn_out, w_out, cos, sin):
    xn = rms_norm(h, norm_g)
    proj = jnp.einsum("bld,de->ble", xn, w_in.astype(xn.dtype))
    (c_val, c_glu, c_z, q, k, v, a_z, g_c, g_a) = jnp.split(proj, IN_OFFSETS, axis=-1)
    y_c = conv_branch(c_val, c_glu, c_z, conv_w, conv_b, cn_g, cn_b, w_conv_out)
    y_a = attn_branch(q, k, v, a_z, q_g, k_g, w_attn_out, cos, sin)
    merged = jax.nn.sigmoid(g_c) * y_c + jax.nn.sigmoid(g_a) * y_a
    return jnp.einsum("bld,de->ble", merged, w_out.astype(merged.dtype))


def setup_inputs(seed: int = 0) -> dict:
    key = jax.random.key(seed)
    ks = jax.random.split(key, 16)
    f32 = jnp.float32
    nrm = lambda k, shape, s: jax.random.normal(k, shape, f32) * s
    return {
        "x": nrm(ks[0], (BATCH, SEQ, D_MODEL), 1.0),
        "meta_tokens": nrm(ks[1], (N_META, D_MODEL), 1.0),
        "norm_g": 1.0 + nrm(ks[2], (DEPTH, D_MODEL), 0.02),
        "w_in": nrm(ks[3], (DEPTH, D_MODEL, IN_DIM), D_MODEL ** -0.5),
        "conv_w": nrm(ks[4], (DEPTH, CONV_K, CONV_DIM), CONV_K ** -0.5),
        "conv_b": nrm(ks[5], (DEPTH, CONV_DIM), 0.02),
        "conv_norm_g": 1.0 + nrm(ks[6], (DEPTH, CONV_DIM), 0.02),
        "conv_norm_b": nrm(ks[7], (DEPTH, CONV_DIM), 0.02),
        "w_conv_out": nrm(ks[8], (DEPTH, CONV_DIM, D_MODEL), CONV_DIM ** -0.5),
        "q_norm_g": 1.0 + nrm(ks[9], (DEPTH, HEAD_DIM), 0.02),
        "k_norm_g": 1.0 + nrm(ks[10], (DEPTH, HEAD_DIM), 0.02),
        "w_attn_out": nrm(ks[11], (DEPTH, ATTN_DIM, D_MODEL), ATTN_DIM ** -0.5),
        "w_out": nrm(ks[12], (DEPTH, D_MODEL, D_MODEL), D_MODEL ** -0.5),
    }


def reference(x, meta_tokens, norm_g, w_in, conv_w, conv_b, conv_norm_g, conv_norm_b,
              w_conv_out, q_norm_g, k_norm_g, w_attn_out, w_out):
    B, n_tok, _ = x.shape
    meta = jnp.broadcast_to(meta_tokens[None].astype(x.dtype), (B, N_META, D_MODEL))
    h = jnp.concatenate([meta, x], axis=1)
    cos, sin = rope_tables(n_tok)
    for layer in range(DEPTH):
        h = h + hybrid_layer(h, norm_g[layer], w_in[layer], conv_w[layer], conv_b[layer],
                             conv_norm_g[layer], conv_norm_b[layer], w_conv_out[layer],
                             q_norm_g[layer], k_norm_g[layer], w_attn_out[layer],
                             w_out[layer], cos, sin)
    return h[:, N_META:]
```

```python
import functools
import math

import jax
import jax.numpy as jnp
import numpy as np
from jax.experimental import pallas as pl
from jax.experimental.pallas import tpu as pltpu

D_MODEL = 1024
N_META = 16
GRID_W = 64
CONV_DIM = 1024
CONV_K = 31
N_HEADS = 16
N_KV_HEADS = 4
HEAD_DIM = 64
GQA_GROUP = N_HEADS // N_KV_HEADS
ATTN_DIM = N_HEADS * HEAD_DIM
KV_DIM = N_KV_HEADS * HEAD_DIM
ROPE_FREQS = HEAD_DIM // 4
ROPE_THETA = 10000.0
NORM_EPS = 1e-6

LANES = 128
SUBLANES = 8
GROUP_LANES = GQA_GROUP * HEAD_DIM
NEG_BIG = -1e30
VMEM_LIMIT = 56 * 1024 * 1024

OFF_VAL, OFF_GLU, OFF_CZ = 0, CONV_DIM, 2 * CONV_DIM
OFF_Q = 3 * CONV_DIM
OFF_K = OFF_Q + ATTN_DIM
OFF_V = OFF_K + KV_DIM
OFF_AZ = OFF_V + KV_DIM
OFF_GC = OFF_AZ + ATTN_DIM
OFF_GA = OFF_GC + D_MODEL
IN_DIM = OFF_GA + D_MODEL

F32 = jnp.float32
BF16 = jnp.bfloat16


def _dot(a, b):
    return jnp.dot(a, b, preferred_element_type=F32)


def _sigmoid(x):
    return jax.nn.sigmoid(x)


def _silu(x):
    return x * jax.nn.sigmoid(x)


def _head_norm_rope(y, n_cols, bd_ref, g_ref, cos_ref, sa_ref, sb_ref, scale):
    outs = []
    for c in range(n_cols // GROUP_LANES):
        yc = y[:, c * GROUP_LANES:(c + 1) * GROUP_LANES]
        sq = yc * yc
        hi = sq.astype(BF16)
        lo = (sq - hi.astype(F32)).astype(BF16)
        ms = _dot(hi, bd_ref[...]) + _dot(lo, bd_ref[...])
        yn = yc * jax.lax.rsqrt(ms + NORM_EPS) * g_ref[:, c * GROUP_LANES:(c + 1) * GROUP_LANES]
        for h in range(GROUP_LANES // LANES):
            t = yn[:, h * LANES:(h + 1) * LANES]
            r = (t * cos_ref[...]
                 + pltpu.roll(t, LANES - ROPE_FREQS, axis=1) * sa_ref[...]
                 + pltpu.roll(t, ROPE_FREQS, axis=1) * sb_ref[...])
            outs.append(r * scale if scale != 1.0 else r)
    return jnp.concatenate(outs, axis=1)


def _inproj_kernel(x_ref, ng_ref, w_ref, cos_ref, sa_ref, sb_ref, qg_ref, kg_ref, bd_ref,
                   u_ref, zc_ref, q_ref, kt_ref, v_ref, za_ref, gc_ref, ga_ref):
    x = x_ref[...]
    ms = jnp.mean(x * x, axis=-1, keepdims=True)
    xn = (x * jax.lax.rsqrt(ms + NORM_EPS) * ng_ref[...]).astype(BF16)

    def seg(lo, width):
        return _dot(xn, w_ref[:, lo:lo + width])

    val = seg(OFF_VAL, CONV_DIM)
    glu = seg(OFF_GLU, CONV_DIM)
    u_ref[...] = (val * _sigmoid(glu)).astype(u_ref.dtype)
    zc_ref[...] = _silu(seg(OFF_CZ, CONV_DIM)).astype(zc_ref.dtype)

    q = _head_norm_rope(seg(OFF_Q, ATTN_DIM), ATTN_DIM, bd_ref, qg_ref, cos_ref, sa_ref, sb_ref,
                        1.0 / math.sqrt(HEAD_DIM))
    q_ref[...] = q.astype(q_ref.dtype)
    k = _head_norm_rope(seg(OFF_K, KV_DIM), KV_DIM, bd_ref, kg_ref, cos_ref, sa_ref, sb_ref, 1.0)
    kt_ref[...] = k.T.astype(kt_ref.dtype)
    v_ref[...] = seg(OFF_V, KV_DIM).astype(v_ref.dtype)

    za_ref[...] = _silu(seg(OFF_AZ, ATTN_DIM)).astype(za_ref.dtype)
    gc_ref[...] = _sigmoid(seg(OFF_GC, D_MODEL)).astype(gc_ref.dtype)
    ga_ref[...] = _sigmoid(seg(OFF_GA, D_MODEL)).astype(ga_ref.dtype)


def _in_projection(x3, ng, w_bf, cos, sa, sb, qg, kg, bd, tm):
    B, T, _ = x3.shape
    nt = T // tm
    row = lambda w: pl.BlockSpec((None, tm, w), lambda t, b: (b, t, 0))
    const = lambda shape: pl.BlockSpec(shape, lambda t, b: (0,) * len(shape))
    tbl = pl.BlockSpec((tm, LANES), lambda t, b: (t, 0))
    wide = jax.ShapeDtypeStruct((B, T, D_MODEL), BF16)
    return pl.pallas_call(
        _inproj_kernel,
        grid=(nt, B),
        in_specs=[row(D_MODEL), const((1, D_MODEL)),
                  pl.BlockSpec((D_MODEL, IN_DIM), lambda t, b: (0, 0), pipeline_mode=pl.Buffered(1)),
                  tbl, tbl, tbl, const((1, ATTN_DIM)), const((1, KV_DIM)),
                  const((GROUP_LANES, GROUP_LANES))],
        out_specs=[row(CONV_DIM), row(CONV_DIM), row(ATTN_DIM),
                   pl.BlockSpec((None, KV_DIM, tm), lambda t, b: (b, 0, t)),
                   row(KV_DIM), row(ATTN_DIM), row(D_MODEL), row(D_MODEL)],
        out_shape=[wide, wide, wide,
                   jax.ShapeDtypeStruct((B, KV_DIM, T), BF16),
                   jax.ShapeDtypeStruct((B, T, KV_DIM), BF16),
                   wide, wide, wide],
        compiler_params=pltpu.CompilerParams(
            dimension_semantics=("arbitrary", "arbitrary"), vmem_limit_bytes=VMEM_LIMIT),
        name="in_projection",
    )(x3, ng, w_bf, cos, sa, sb, qg, kg, bd)


HALO = 16


def _conv_kernel(u_ref, prev_ref, next_ref, umeta_ref, zc_ref, w_ref, b_ref, g_ref, beta_ref,
                 o_ref, ext_ref, *, tc):
    t = pl.program_id(1)
    nt = pl.num_programs(1)
    prev = jnp.where(t == 0, umeta_ref[...], prev_ref[...]).astype(F32)
    nxt = jnp.where(t == nt - 1, jnp.zeros_like(next_ref[...]), next_ref[...]).astype(F32)
    ext_ref[0:HALO, :] = prev
    ext_ref[HALO:HALO + tc, :] = u_ref[...].astype(F32)
    ext_ref[HALO + tc:HALO + tc + HALO, :] = nxt

    base = HALO - CONV_K // 2
    n_a = (base + CONV_K - 1) // SUBLANES + 1
    acc = jnp.zeros((tc, CONV_DIM), F32)
    for r in range(SUBLANES):
        win = ext_ref[pl.ds(r, tc + (n_a - 1) * SUBLANES), :]
        for a in range(n_a):
            k = a * SUBLANES + r - base
            if 0 <= k < CONV_K:
                acc = acc + win[a * SUBLANES:a * SUBLANES + tc, :] * w_ref[k:k + 1, :]
    c = acc + b_ref[...]
    mu = jnp.mean(c, axis=-1, keepdims=True)
    xc = c - mu
    var = jnp.mean(xc * xc, axis=-1, keepdims=True)
    y = xc * jax.lax.rsqrt(var + NORM_EPS) * g_ref[...] + beta_ref[...]
    o_ref[...] = (_silu(y) * zc_ref[...].astype(F32)).astype(o_ref.dtype)


def _conv_branch(u, u_meta, zc, conv_w, conv_b, cn_g, cn_b, tc):
    B, T, C = u.shape
    nt = T // tc
    hb = tc // HALO
    n_hblk = T // HALO
    main = pl.BlockSpec((None, tc, C), lambda b, t: (b, t, 0))
    const = lambda shape: pl.BlockSpec(shape, lambda b, t: (0,) * len(shape))
    return pl.pallas_call(
        functools.partial(_conv_kernel, tc=tc),
        grid=(B, nt),
        in_specs=[main,
                  pl.BlockSpec((None, HALO, C), lambda b, t: (b, jnp.maximum(t * hb - 1, 0), 0)),
                  pl.BlockSpec((None, HALO, C),
                               lambda b, t: (b, jnp.minimum((t + 1) * hb, n_hblk - 1), 0)),
                  const((HALO, C)), main, const((CONV_K, C)), const((1, C)), const((1, C)),
                  const((1, C))],
        out_specs=main,
        out_shape=jax.ShapeDtypeStruct((B, T, C), BF16),
        scratch_shapes=[pltpu.VMEM((tc + 2 * HALO, C), F32)],
        compiler_params=pltpu.CompilerParams(
            dimension_semantics=("arbitrary", "arbitrary"), vmem_limit_bytes=VMEM_LIMIT),
        name="conv_branch",
    )(u, u, u, u_meta, zc, conv_w, conv_b, cn_g, cn_b)


def _attn_kernel(q_ref, kt_ref, ktm_ref, v_ref, vm_ref, bias_ref, za_ref, o_ref, *, tq):
    for g in range(N_KV_HEADS):
        c0 = g * GROUP_LANES
        qg = q_ref[:, c0:c0 + GROUP_LANES]
        qs = jnp.concatenate([qg[:, j * HEAD_DIM:(j + 1) * HEAD_DIM] for j in range(GQA_GROUP)],
                             axis=0)
        r0 = g * HEAD_DIM
        s_main = _dot(qs, kt_ref[r0:r0 + HEAD_DIM, :])
        s_tail = _dot(qs, ktm_ref[r0:r0 + HEAD_DIM, :]) + bias_ref[...]
        m = jnp.maximum(jnp.max(s_main, axis=-1, keepdims=True),
                        jnp.max(s_tail, axis=-1, keepdims=True))
        p_main = jnp.exp(s_main - m)
        p_tail = jnp.exp(s_tail - m)
        l = jnp.sum(p_main, axis=-1, keepdims=True) + jnp.sum(p_tail, axis=-1, keepdims=True)
        o = (_dot(p_main.astype(BF16), v_ref[:, r0:r0 + HEAD_DIM])
             + _dot(p_tail.astype(BF16), vm_ref[:, r0:r0 + HEAD_DIM]))
        o = o / l
        og = jnp.concatenate([o[j * tq:(j + 1) * tq, :] for j in range(GQA_GROUP)], axis=1)
        o_ref[:, c0:c0 + GROUP_LANES] = (
            og * za_ref[:, c0:c0 + GROUP_LANES].astype(F32)).astype(o_ref.dtype)


def _attention(q, kt, kt_meta, v, v_meta, bias, za, tq):
    B, T, _ = q.shape
    nq = T // tq
    row = pl.BlockSpec((None, tq, ATTN_DIM), lambda b, i: (b, i, 0))
    const = lambda shape: pl.BlockSpec(shape, lambda b, i: (0,) * len(shape))
    return pl.pallas_call(
        functools.partial(_attn_kernel, tq=tq),
        grid=(B, nq),
        in_specs=[row,
                  pl.BlockSpec((None, KV_DIM, T), lambda b, i: (b, 0, 0)),
                  const((KV_DIM, LANES)),
                  pl.BlockSpec((None, T, KV_DIM), lambda b, i: (b, 0, 0)),
                  const((LANES, KV_DIM)), const((1, LANES)), row],
        out_specs=row,
        out_shape=jax.ShapeDtypeStruct((B, T, ATTN_DIM), BF16),
        compiler_params=pltpu.CompilerParams(
            dimension_semantics=("arbitrary", "arbitrary"), vmem_limit_bytes=VMEM_LIMIT),
        name="attention",
    )(q, kt, kt_meta, v, v_meta, bias, za)


def _out_kernel(c_ref, o_ref, gc_ref, ga_ref, x_ref, wc_ref, wa_ref, wo_ref, out_ref):
    y_c = _dot(c_ref[...], wc_ref[...])
    y_a = _dot(o_ref[...], wa_ref[...])
    merged = gc_ref[...].astype(F32) * y_c + ga_ref[...].astype(F32) * y_a
    out_ref[...] = x_ref[...] + _dot(merged.astype(BF16), wo_ref[...])


def _out_projection(c_act, o_gated, gc, ga, x3, wc, wa, wo, tm):
    B, T, D = x3.shape
    nt = T // tm
    row = pl.BlockSpec((None, tm, D), lambda b, t: (b, t, 0))
    wspec = pl.BlockSpec((D, D), lambda b, t: (0, 0))
    return pl.pallas_call(
        _out_kernel,
        grid=(B, nt),
        in_specs=[row, row, row, row, row, wspec, wspec, wspec],
        out_specs=row,
        out_shape=jax.ShapeDtypeStruct((B, T, D), F32),
        compiler_params=pltpu.CompilerParams(
            dimension_semantics=("arbitrary", "arbitrary"), vmem_limit_bytes=VMEM_LIMIT),
        name="out_projection",
    )(c_act, o_gated, gc, ga, x3, wc, wa, wo)


def _rope_tables(n_tok):
    rows = n_tok // GRID_W
    row_ids = jnp.repeat(jnp.arange(rows, dtype=F32), GRID_W)
    col_ids = jnp.tile(jnp.arange(GRID_W, dtype=F32), rows)
    inv_freq = ROPE_THETA ** (-jnp.arange(ROPE_FREQS, dtype=F32) / ROPE_FREQS)
    a_row = row_ids[:, None] * inv_freq[None, :]
    a_col = col_ids[:, None] * inv_freq[None, :]
    ang = jnp.concatenate([a_row, a_row, a_col, a_col], axis=-1)
    cos, sin = jnp.cos(ang), jnp.sin(ang)
    first = (np.arange(HEAD_DIM) % (2 * ROPE_FREQS)) < ROPE_FREQS
    sa = jnp.where(first[None, :], -sin, 0.0)
    sb = jnp.where(first[None, :], 0.0, sin)
    two = lambda t: jnp.tile(t, (1, LANES // HEAD_DIM))
    return two(cos), two(sa), two(sb)


def kernel(x, meta_tokens, norm_g, w_in, conv_w, conv_b, conv_norm_g, conv_norm_b, w_conv_out,
           q_norm_g, k_norm_g, w_attn_out, w_out):
    B, T, D = x.shape
    assert norm_g.shape[0] == 1, "single-layer block"
    assert D == D_MODEL and T % GRID_W == 0 and N_META == HALO

    ng = norm_g[0].reshape(1, D)
    w_bf = w_in[0].astype(BF16)
    qg = jnp.tile(q_norm_g[0], N_HEADS).reshape(1, ATTN_DIM)
    kg = jnp.tile(k_norm_g[0], N_KV_HEADS).reshape(1, KV_DIM)
    head = np.arange(GROUP_LANES) // HEAD_DIM
    bd = jnp.asarray((head[:, None] == head[None, :]) / HEAD_DIM, dtype=BF16)
    cos, sa, sb = _rope_tables(T)
    cos_m = jnp.ones((N_META, LANES), F32)
    zero_m = jnp.zeros((N_META, LANES), F32)

    u, zc, q, kt, v, za, gc, ga = _in_projection(x, ng, w_bf, cos, sa, sb, qg, kg, bd, tm=512)
    meta_out = _in_projection(meta_tokens[None].astype(x.dtype), ng, w_bf, cos_m, zero_m, zero_m,
                              qg, kg, bd, tm=N_META)
    u_meta, kt_meta, v_meta = meta_out[0][0], meta_out[3][0], meta_out[4][0]

    c_act = _conv_branch(u, u_meta, zc, conv_w[0], conv_b[0].reshape(1, -1),
                         conv_norm_g[0].reshape(1, -1), conv_norm_b[0].reshape(1, -1), tc=256)

    kt_meta_p = jnp.pad(kt_meta, ((0, 0), (0, LANES - N_META)))
    v_meta_p = jnp.pad(v_meta, ((0, LANES - N_META), (0, 0)))
    bias = jnp.where(jnp.arange(LANES) < N_META, 0.0, NEG_BIG).astype(F32).reshape(1, LANES)
    o_gated = _attention(q, kt, kt_meta_p, v, v_meta_p, bias, za, tq=256)

    return _out_projection(c_act, o_gated, gc, ga, x, w_conv_out[0].astype(BF16),
                           w_attn_out[0].astype(BF16), w_out[0].astype(BF16), tm=512)
```
